```python
import jax, jax.numpy as jnp
from jax import lax
import numpy as np

D_MODEL = 2048
BATCH = 4
SEQ = 4096
DEPTH = 2

CHUNK = 64
W_A = D_MODEL // 2
W_B = D_MODEL // 2
D_MIX = W_A + W_B
H_A = 8
D_K = W_A // H_A
D_V = W_A // H_A
CONV_WIDTH = 31
N_IN = 4 * W_A + 3 * W_B
EPS = 1e-6
LB_FLOOR = 1e-30

kernel_name = "hgrn2_conformer_hybrid_trunk"


def rmsnorm(x, g):
    xf = x.astype(jnp.float32)
    y = xf * lax.rsqrt(jnp.mean(xf * xf, axis=-1, keepdims=True) + EPS)
    return (y * g.astype(jnp.float32)).astype(x.dtype)


def layernorm(x, g, b):
    xf = x.astype(jnp.float32)
    mu = jnp.mean(xf, axis=-1, keepdims=True)
    var = jnp.mean(jnp.square(xf - mu), axis=-1, keepdims=True)
    y = (xf - mu) * lax.rsqrt(var + EPS)
    return (y * g.astype(jnp.float32) + b.astype(jnp.float32)).astype(x.dtype)


def layer_lower_bounds(lb_logits):
    p = jax.nn.softmax(lb_logits.astype(jnp.float32), axis=0)
    return jnp.cumsum(p, axis=0) - p[0:1]


def hgrn2_mixer(q, fz, iv, lb, norm_g):
    B, T, _ = q.shape
    nc = T // CHUNK

    def heads(u, d):
        return u.astype(jnp.float32).reshape(B, nc, CHUNK, H_A, d).transpose(1, 0, 3, 2, 4)

    lb_h = lb.astype(jnp.float32).reshape(H_A, 1, D_K)
    zq = heads(q, D_K) * (D_K ** -0.5)
    zf = heads(fz, D_K)
    v = heads(iv, D_V)
    log_lb = jnp.log(jnp.maximum(lb_h, LB_FLOOR))
    log_f = jnp.logaddexp(log_lb, jnp.log1p(-lb_h) + jax.nn.log_sigmoid(zf))
    k = (1.0 - lb_h) * jax.nn.sigmoid(-zf)
    causal = jnp.tril(jnp.ones((CHUNK, CHUNK), dtype=bool))[:, :, None]

    def step(S, inp):
        qc, kc, vc, gc = inp
        b = jnp.cumsum(gc, axis=-2)
        diff = b[..., :, None, :] - b[..., None, :, :]
        decay = jnp.where(causal, jnp.exp(jnp.where(causal, diff, 0.0)), 0.0)
        scores = jnp.einsum('bhtk,bhtsk,bhsk->bhts', qc, decay, kc)
        o = jnp.einsum('bhts,bhsv->bhtv', scores, vc) \
            + jnp.einsum('bhtk,bhkv->bhtv', qc * jnp.exp(b), S)
        b_last = b[..., -1:, :]
        S_new = jnp.exp(b_last[..., 0, :])[..., None] * S \
            + jnp.einsum('bhsk,bhsv->bhkv', kc * jnp.exp(b_last - b), vc)
        return S_new, o

    S0 = jnp.zeros((B, H_A, D_K, D_V), jnp.float32)
    _, o = lax.scan(step, S0, (zq, k, v, log_f))
    o = o.transpose(1, 0, 3, 2, 4).reshape(B, T, H_A, D_V)
    o = o * lax.rsqrt(jnp.mean(o * o, axis=-1, keepdims=True) + EPS) * norm_g.astype(jnp.float32)
    return o.reshape(B, T, W_A).astype(q.dtype)


def conformer_conv(u, ug, dw_k, dw_b, ln_g, ln_b):
    a = u * jax.nn.sigmoid(ug)
    y = lax.conv_general_dilated(
        a, dw_k.astype(a.dtype)[:, None, :],
        window_strides=(1,), padding=[(CONV_WIDTH - 1, 0)],
        dimension_numbers=('NWC', 'WIO', 'NWC'), feature_group_count=W_B)
    y = y + dw_b.astype(y.dtype)
    y = layernorm(y, ln_g, ln_b)
    return jax.nn.silu(y)


def setup_inputs(seed: int = 0) -> dict:
    key = jax.random.key(seed)
    ks = jax.random.split(key, 12)
    f32 = jnp.float32
    x = jax.random.normal(ks[0], (BATCH, SEQ, D_MODEL), f32)
    pre_norm_g = 1.0 + 0.05 * jax.random.normal(ks[1], (DEPTH, D_MODEL), f32)
    post_norm_g = 1.0 + 0.05 * jax.random.normal(ks[2], (DEPTH, D_MODEL), f32)
    w_in = jax.random.normal(ks[3], (DEPTH, D_MODEL, N_IN), f32) * D_MODEL ** -0.5
    lb_logits = 0.5 * jax.random.normal(ks[4], (DEPTH, W_A), f32)
    hg_norm_g = 1.0 + 0.05 * jax.random.normal(ks[5], (DEPTH, D_V), f32)
    dw_kernel = jax.random.normal(ks[6], (DEPTH, CONV_WIDTH, W_B), f32) * CONV_WIDTH ** -0.5
    dw_bias = 0.02 * jax.random.normal(ks[7], (DEPTH, W_B), f32)
    conv_ln_g = 1.0 + 0.05 * jax.random.normal(ks[8], (DEPTH, W_B), f32)
    conv_ln_b = 0.02 * jax.random.normal(ks[9], (DEPTH, W_B), f32)
    w_out = jax.random.normal(ks[10], (DEPTH, D_MIX, D_MODEL), f32) * D_MIX ** -0.5
    return {"x": x, "pre_norm_g": pre_norm_g, "post_norm_g": post_norm_g, "w_in": w_in,
            "lb_logits": lb_logits, "hg_norm_g": hg_norm_g, "dw_kernel": dw_kernel,
            "dw_bias": dw_bias, "conv_ln_g": conv_ln_g, "conv_ln_b": conv_ln_b, "w_out": w_out}


def reference(x, pre_norm_g, post_norm_g, w_in, lb_logits, hg_norm_g, dw_kernel,
              dw_bias, conv_ln_g, conv_ln_b, w_out):
    lower_bounds = layer_lower_bounds(lb_logits)
    split_pts = [W_A, 2 * W_A, 3 * W_A, 4 * W_A, 4 * W_A + W_B, 4 * W_A + 2 * W_B]
    for l in range(DEPTH):
        h = rmsnorm(x, pre_norm_g[l])
        proj = jnp.einsum('btd,dn->btn', h, w_in[l].astype(h.dtype))
        q, fz, iv, gate_a, u, ug, gate_b = jnp.split(proj, split_pts, axis=-1)
        o_a = hgrn2_mixer(q, fz, iv, lower_bounds[l], hg_norm_g[l]) * jax.nn.silu(gate_a)
        o_b = conformer_conv(u, ug, dw_kernel[l], dw_bias[l], conv_ln_g[l], conv_ln_b[l]) \
            * jax.nn.silu(gate_b)
        mixed = jnp.concatenate([o_a, o_b], axis=-1)
        y = jnp.einsum('btm,md->btd', mixed, w_out[l].astype(mixed.dtype))
        x = x + rmsnorm(y, post_norm_g[l])
    return x
```

```python
import functools

import jax
import jax.numpy as jnp
from jax import lax
from jax.experimental import pallas as pl
from jax.experimental.pallas import tpu as pltpu

H_A = 8
D_HEAD = 128
CONV_WIDTH = 31
EPS = 1e-6
LB_FLOOR = 1e-30
CHUNK = 64
SUB = 8
LEVELS = (8, 16, 32)

F32 = jnp.float32
BF16 = jnp.bfloat16
VMEM_LIMIT = 56 * 1024 * 1024


def _sigmoid_pair(z):
    e = jnp.exp(-jnp.abs(z))
    r = 1.0 / (1.0 + e)
    er = e * r
    pos = z >= 0
    return jnp.where(pos, r, er), jnp.where(pos, er, r)


def _silu(x):
    return x * (1.0 / (1.0 + jnp.exp(-x)))


def _in_proj_kernel(x_ref, g_ref, w_ref, o_ref, h_ref, *, row_chunk):
    tm = x_ref.shape[0]

    @pl.when(pl.program_id(1) == 0)
    def _():
        def body(r, carry):
            rs = pl.ds(pl.multiple_of(r * row_chunk, row_chunk), row_chunk)
            x = x_ref[rs, :]
            ms = jnp.mean(x * x, axis=-1, keepdims=True)
            h_ref[rs, :] = (x * lax.rsqrt(ms + EPS) * g_ref[...]).astype(BF16)
            return carry
        lax.fori_loop(0, tm // row_chunk, body, 0)

    o_ref[...] = jnp.dot(h_ref[...], w_ref[...], preferred_element_type=F32).astype(o_ref.dtype)


def _in_proj(x2, g, w_bf16, *, tm, tn):
    m, d = x2.shape
    n = w_bf16.shape[1]
    return pl.pallas_call(
        functools.partial(_in_proj_kernel, row_chunk=32),
        out_shape=jax.ShapeDtypeStruct((m, n), BF16),
        grid=(m // tm, n // tn),
        in_specs=[
            pl.BlockSpec((tm, d), lambda i, j: (i, 0)),
            pl.BlockSpec((1, d), lambda i, j: (0, 0)),
            pl.BlockSpec((d, tn), lambda i, j: (0, j)),
        ],
        out_specs=pl.BlockSpec((tm, tn), lambda i, j: (i, j)),
        scratch_shapes=[pltpu.VMEM((tm, d), BF16)],
        compiler_params=pltpu.CompilerParams(
            dimension_semantics=("parallel", "arbitrary"), vmem_limit_bytes=VMEM_LIMIT),
        name="in_proj",
    )(x2, g, w_bf16)


def _group_first(b, size):
    c, d = b.shape
    b3 = b.reshape(c // size, size, d)
    return jnp.broadcast_to(b3[:, 0:1, :], b3.shape).reshape(c, d)


def _hgrn_chunk(q, z, v_bf, lbf, oml, st, masks):
    c = q.shape[0]
    tril, level_masks, diag_masks = masks
    s_pos, s_neg = _sigmoid_pair(z)
    g = jnp.log(lbf + oml * s_pos)
    k = oml * s_neg
    g1 = g.astype(BF16)
    r1 = g - g1.astype(F32)
    g2 = r1.astype(BF16)
    g3 = (r1 - g2.astype(F32)).astype(BF16)
    b = (jnp.dot(tril, g1, preferred_element_type=F32)
         + jnp.dot(tril, g2, preferred_element_type=F32)
         + jnp.dot(tril, g3, preferred_element_type=F32))

    a = None
    b_last = b[c - 1:c, :]
    for size, mask in zip(LEVELS, level_masks):
        rowref = _group_first(b, size)
        colref = jnp.concatenate(
            [rowref[size:, :], jnp.broadcast_to(b_last, (size, b.shape[1]))], axis=0)
        qd = (q * jnp.exp(b - rowref)).astype(BF16)
        kd = (k * jnp.exp(colref - b)).astype(BF16)
        m = lax.dot_general(qd, kd, (((1,), (1,)), ((), ())), preferred_element_type=F32)
        a = jnp.where(mask, m, 0.0) if a is None else jnp.where(mask, m, a)

    d = q.shape[1]
    q3 = q.reshape(c // SUB, SUB, d)
    k3 = k.reshape(c // SUB, SUB, d)
    b3 = b.reshape(c // SUB, SUB, d)
    for s in range(SUB):
        kb = jnp.broadcast_to(k3[:, s:s + 1, :], k3.shape)
        bb = jnp.broadcast_to(b3[:, s:s + 1, :], b3.shape)
        p = q3 * kb * jnp.exp(b3 - bb)
        col = jnp.sum(p, axis=-1, keepdims=True).reshape(c, 1)
        a = jnp.where(diag_masks[s], col, a)

    o = jnp.dot(a.astype(BF16), v_bf, preferred_element_type=F32)
    qi = (q * jnp.exp(b)).astype(BF16)
    o = o + lax.dot_general(qi, st.astype(BF16), (((1,), (1,)), ((), ())), preferred_element_type=F32)
    kdec = (k * jnp.exp(b_last - b)).astype(BF16)
    upd = lax.dot_general(v_bf, kdec, (((0,), (0,)), ((), ())), preferred_element_type=F32)
    st_new = st * jnp.exp(b_last) + upd
    return o, st_new


def _hgrn_masks(c):
    row = lax.broadcasted_iota(jnp.int32, (c, c), 0)
    col = lax.broadcasted_iota(jnp.int32, (c, c), 1)
    tril = (col <= row).astype(BF16)
    level_masks = []
    for size in LEVELS:
        rb = row // size
        cb = col // size
        level_masks.append(((rb % 2) == 1) & (cb == rb - 1))
    diag_masks = []
    for s in range(SUB):
        diag_masks.append((col == (row // SUB) * SUB + s) & ((row % SUB) >= s))
    return tril, level_masks, diag_masks


def _hgrn_kernel(lbl_ref, q_ref, f_ref, v_ref, gate_ref, ng_ref, o_ref, st_ref, *, layer):
    tb = q_ref.shape[0]

    @pl.when(pl.program_id(2) == 0)
    def _():
        st_ref[...] = jnp.zeros_like(st_ref)

    lg = lbl_ref[...].astype(F32)
    e = jnp.exp(lg - jnp.max(lg, axis=0, keepdims=True))
    p = e / jnp.sum(e, axis=0, keepdims=True)
    acc = p[0:1, :]
    for i in range(1, layer + 1):
        acc = acc + p[i:i + 1, :]
    lb = acc - p[0:1, :]
    lbf = jnp.maximum(lb, LB_FLOOR)
    oml = 1.0 - lb

    masks = _hgrn_masks(CHUNK)
    ng = ng_ref[...].astype(F32)
    st = st_ref[...]
    for ci in range(tb // CHUNK):
        rows = pl.ds(ci * CHUNK, CHUNK)
        q = q_ref[rows, :].astype(F32) * (D_HEAD ** -0.5)
        z = f_ref[rows, :].astype(F32)
        o, st = _hgrn_chunk(q, z, v_ref[rows, :], lbf, oml, st, masks)
        ms = jnp.mean(o * o, axis=-1, keepdims=True)
        on = o * lax.rsqrt(ms + EPS) * ng
        o_ref[rows, :] = (on * _silu(gate_ref[rows, :].astype(F32))).astype(o_ref.dtype)
    st_ref[...] = st


def _hgrn(proj, lb_logits, norm_g, *, layer, batch, seq, tb):
    nt = seq // tb
    depth = lb_logits.shape[0]

    def col_spec(slot):
        return pl.BlockSpec((tb, D_HEAD), lambda b, h, t: (b * nt + t, slot * H_A + h))

    return pl.pallas_call(
        functools.partial(_hgrn_kernel, layer=layer),
        out_shape=jax.ShapeDtypeStruct((batch * seq, H_A * D_HEAD), BF16),
        grid=(batch, H_A, nt),
        in_specs=[
            pl.BlockSpec((depth, D_HEAD), lambda b, h, t: (0, h)),
            col_spec(0), col_spec(1), col_spec(2), col_spec(3),
            pl.BlockSpec((1, D_HEAD), lambda b, h, t: (0, 0)),
        ],
        out_specs=pl.BlockSpec((tb, D_HEAD), lambda b, h, t: (b * nt + t, h)),
        scratch_shapes=[pltpu.VMEM((D_HEAD, D_HEAD), F32)],
        compiler_params=pltpu.CompilerParams(
            dimension_semantics=("parallel", "parallel", "arbitrary"), vmem_limit_bytes=VMEM_LIMIT),
        name="hgrn2",
    )(lb_logits, proj, proj, proj, proj, norm_g)


HIST = 32
LANES = 128
STRIP = 64


def _conv_kernel(u_ref, ug_ref, gate_ref, w_ref, bias_ref, lng_ref, lnb_ref, o_ref, a_ref, y_ref):
    tb, wb = u_ref.shape

    @pl.when(pl.program_id(1) == 0)
    def _():
        a_ref[0:HIST, :] = jnp.zeros((HIST, wb), F32)

    u = u_ref[...].astype(F32)
    ug = ug_ref[...].astype(F32)
    a_ref[HIST:HIST + tb, :] = u * (1.0 / (1.0 + jnp.exp(-ug)))

    first = HIST - (CONV_WIDTH - 1)
    for r0 in range(0, tb, STRIP):
        s1 = jnp.zeros((STRIP, LANES), F32)
        for cs in range(0, wb, LANES):
            acc = jnp.broadcast_to(bias_ref[0:1, cs:cs + LANES], (STRIP, LANES)).astype(F32)
            for j in range(CONV_WIDTH):
                acc = acc + w_ref[j:j + 1, cs:cs + LANES] * a_ref[r0 + first + j:r0 + first + j + STRIP, cs:cs + LANES]
            y_ref[r0:r0 + STRIP, cs:cs + LANES] = acc
            s1 = s1 + acc
        mu = jnp.sum(s1, axis=-1, keepdims=True) * (1.0 / wb)
        yc = y_ref[r0:r0 + STRIP, :] - mu
        var = jnp.mean(yc * yc, axis=-1, keepdims=True)
        yn = yc * lax.rsqrt(var + EPS) * lng_ref[...] + lnb_ref[...]
        gate = gate_ref[r0:r0 + STRIP, :].astype(F32)
        o_ref[r0:r0 + STRIP, :] = (_silu(yn) * _silu(gate)).astype(o_ref.dtype)

    a_ref[0:HIST, :] = a_ref[tb:tb + HIST, :]


def _conv(proj, w, bias, ln_g, ln_b, *, batch, seq, tb):
    nt = seq // tb
    wb = w.shape[1]
    base = 4 * H_A * D_HEAD // wb

    def col_spec(slot):
        return pl.BlockSpec((tb, wb), lambda b, t: (b * nt + t, base + slot))

    def full(shape):
        return pl.BlockSpec(shape, lambda b, t: (0, 0))

    return pl.pallas_call(
        _conv_kernel,
        out_shape=jax.ShapeDtypeStruct((batch * seq, wb), BF16),
        grid=(batch, nt),
        in_specs=[col_spec(0), col_spec(1), col_spec(2),
                  full((CONV_WIDTH, wb)), full((1, wb)), full((1, wb)), full((1, wb))],
        out_specs=pl.BlockSpec((tb, wb), lambda b, t: (b * nt + t, 0)),
        scratch_shapes=[pltpu.VMEM((HIST + tb, wb), F32), pltpu.VMEM((tb, wb), F32)],
        compiler_params=pltpu.CompilerParams(
            dimension_semantics=("parallel", "arbitrary"), vmem_limit_bytes=VMEM_LIMIT),
        name="conf_conv",
    )(proj, proj, proj, w, bias, ln_g, ln_b)


def _out_proj_kernel(oa_ref, ob_ref, wa_ref, wb_ref, x_ref, g_ref, o_ref):
    y = jnp.dot(oa_ref[...], wa_ref[...], preferred_element_type=F32)
    y = y + jnp.dot(ob_ref[...], wb_ref[...], preferred_element_type=F32)
    ms = jnp.mean(y * y, axis=-1, keepdims=True)
    o_ref[...] = x_ref[...] + y * lax.rsqrt(ms + EPS) * g_ref[...]


def _out_proj(oa, ob, w_bf16, x2, g, *, tm):
    m, d = x2.shape
    wa = oa.shape[1]
    wb = ob.shape[1]
    return pl.pallas_call(
        _out_proj_kernel,
        out_shape=jax.ShapeDtypeStruct((m, d), F32),
        grid=(m // tm,),
        in_specs=[
            pl.BlockSpec((tm, wa), lambda i: (i, 0)),
            pl.BlockSpec((tm, wb), lambda i: (i, 0)),
            pl.BlockSpec((wa, d), lambda i: (0, 0)),
            pl.BlockSpec((wb, d), lambda i: (wa // wb, 0)),
            pl.BlockSpec((tm, d), lambda i: (i, 0)),
            pl.BlockSpec((1, d), lambda i: (0, 0)),
        ],
        out_specs=pl.BlockSpec((tm, d), lambda i: (i, 0)),
        compiler_params=pltpu.CompilerParams(
            dimension_semantics=("parallel",), vmem_limit_bytes=VMEM_LIMIT),
        name="out_proj",
    )(oa, ob, w_bf16, w_bf16, x2, g)


def _tiles(batch, seq):
    m = batch * seq
    return dict(
        tm_in=min(1024, m), tn_in=1024,
        tb_hgrn=min(512, seq), tb_conv=min(128, seq),
        tm_out=min(512, m),
    )


def kernel(x, pre_norm_g, post_norm_g, w_in, lb_logits, hg_norm_g, dw_kernel, dw_bias, conv_ln_g,
           conv_ln_b, w_out):
    batch, seq, d = x.shape
    depth = w_in.shape[0]
    t = _tiles(batch, seq)
    x2 = x.reshape(batch * seq, d)
    for l in range(depth):
        proj = _in_proj(x2, pre_norm_g[l:l + 1], w_in[l].astype(BF16), tm=t["tm_in"], tn=t["tn_in"])
        oa = _hgrn(proj, lb_logits, hg_norm_g[l:l + 1], layer=l, batch=batch, seq=seq, tb=t["tb_hgrn"])
        ob = _conv(proj, dw_kernel[l], dw_bias[l:l + 1], conv_ln_g[l:l + 1], conv_ln_b[l:l + 1],
                   batch=batch, seq=seq, tb=t["tb_conv"])
        x2 = _out_proj(oa, ob, w_out[l].astype(BF16), x2, post_norm_g[l:l + 1], tm=t["tm_out"])
    return x2.reshape(batch, seq, d)
```

```python
import functools

import jax
import jax.numpy as jnp
from jax import lax
from jax.experimental import pallas as pl
from jax.experimental.pallas import tpu as pltpu

H_A = 8
D_HEAD = 128
CONV_WIDTH = 31
EPS = 1e-6
LB_FLOOR = 1e-30
CHUNK = 64
SUB = 8
LEVELS = (8, 16, 32)

F32 = jnp.float32
BF16 = jnp.bfloat16
VMEM_LIMIT = 56 * 1024 * 1024


def _sigmoid_pair(z):
    e = jnp.exp(-jnp.abs(z))
    r = 1.0 / (1.0 + e)
    er = e * r
    pos = z >= 0
    return jnp.where(pos, r, er), jnp.where(pos, er, r)


def _silu(x):
    return x * (1.0 / (1.0 + jnp.exp(-x)))


def _in_proj_kernel(x_ref, g_ref, w_ref, o_ref, h_ref, *, row_chunk):
    tm = x_ref.shape[0]

    @pl.when(pl.program_id(1) == 0)
    def _():
        def body(r, carry):
            rs = pl.ds(pl.multiple_of(r * row_chunk, row_chunk), row_chunk)
            x = x_ref[rs, :]
            ms = jnp.mean(x * x, axis=-1, keepdims=True)
            h_ref[rs, :] = (x * lax.rsqrt(ms + EPS) * g_ref[...]).astype(BF16)
            return carry
        lax.fori_loop(0, tm // row_chunk, body, 0)

    o_ref[...] = jnp.dot(h_ref[...], w_ref[...], preferred_element_type=F32).astype(o_ref.dtype)


def _in_proj(x2, g, w_bf16, *, tm, tn):
    m, d = x2.shape
    n = w_bf16.shape[1]
    return pl.pallas_call(
        functools.partial(_in_proj_kernel, row_chunk=32),
        out_shape=jax.ShapeDtypeStruct((m, n), BF16),
        grid=(m // tm, n // tn),
        in_specs=[
            pl.BlockSpec((tm, d), lambda i, j: (i, 0)),
            pl.BlockSpec((1, d), lambda i, j: (0, 0)),
            pl.BlockSpec((d, tn), lambda i, j: (0, j)),
        ],
        out_specs=pl.BlockSpec((tm, tn), lambda i, j: (i, j)),
        scratch_shapes=[pltpu.VMEM((tm, d), BF16)],
        compiler_params=pltpu.CompilerParams(
            dimension_semantics=("parallel", "arbitrary"), vmem_limit_bytes=VMEM_LIMIT),
        name="in_proj",
    )(x2, g, w_bf16)


NT_DIMS = (((1,), (1,)), ((), ()))
TN_DIMS = (((0,), (0,)), ((), ()))


def _bcast_row(ref, row, nrows):
    return jnp.broadcast_to(ref[pl.ds(row, 1), :], (nrows, ref.shape[1]))


def _hgrn_masks(c):
    row = lax.broadcasted_iota(jnp.int32, (c, c), 0)
    col = lax.broadcasted_iota(jnp.int32, (c, c), 1)
    tril = (col <= row).astype(BF16)
    level_masks = {}
    for size in LEVELS:
        rb = row // size
        level_masks[size] = ((rb % 2) == 1) & (col // size == rb - 1)
    diag_masks = [(col == (row // SUB) * SUB + s) & ((row % SUB) >= s) for s in range(SUB)]
    return tril, level_masks, diag_masks


def _hgrn_kernel(lbl_ref, q_ref, f_ref, v_ref, gate_ref, ng_ref, o_ref, st_ref, b_ref, k_ref, *, layer):
    tb, d = q_ref.shape
    nc = tb // CHUNK

    @pl.when(pl.program_id(2) == 0)
    def _():
        st_ref[...] = jnp.zeros_like(st_ref)

    lg = lbl_ref[...].astype(F32)
    e = jnp.exp(lg - jnp.max(lg, axis=0, keepdims=True))
    p = e / jnp.sum(e, axis=0, keepdims=True)
    acc = p[0:1, :]
    for i in range(1, layer + 1):
        acc = acc + p[i:i + 1, :]
    lb = acc - p[0:1, :]
    lbf = jnp.maximum(lb, LB_FLOOR)
    oml = 1.0 - lb

    tril, level_masks, diag_masks = _hgrn_masks(CHUNK)
    chunks = [pl.ds(c * CHUNK, CHUNK) for c in range(nc)]

    gcat = []
    for rows in chunks:
        s_pos, s_neg = _sigmoid_pair(f_ref[rows, :].astype(F32))
        g = jnp.log(lbf + oml * s_pos)
        k_ref[rows, :] = oml * s_neg
        g1 = g.astype(BF16)
        r1 = g - g1.astype(F32)
        g2 = r1.astype(BF16)
        g3 = (r1 - g2.astype(F32)).astype(BF16)
        gcat.append(jnp.concatenate([g1, g2, g3], axis=1))
    for rows, gc in zip(chunks, gcat):
        e3 = jnp.dot(tril, gc, preferred_element_type=F32)
        b_ref[rows, :] = e3[:, 0:d] + e3[:, d:2 * d] + e3[:, 2 * d:3 * d]

    zeros = {size: jnp.zeros((size, d), F32) for size in LEVELS}
    level = {}
    for size in LEVELS:
        for c in range(nc):
            qd, kd = [], []
            for gi in range(CHUNK // size):
                r0 = c * CHUNK + gi * size
                grp = pl.ds(r0, size)
                if gi % 2 == 1:
                    dec = jnp.exp(b_ref[grp, :] - _bcast_row(b_ref, r0 - 1, size))
                    qd.append(q_ref[grp, :].astype(F32) * dec)
                    kd.append(zeros[size])
                else:
                    dec = jnp.exp(_bcast_row(b_ref, r0 + size - 1, size) - b_ref[grp, :])
                    kd.append(k_ref[grp, :] * dec)
                    qd.append(zeros[size])
            qd = jnp.concatenate(qd, axis=0).astype(BF16)
            kd = jnp.concatenate(kd, axis=0).astype(BF16)
            level[size, c] = lax.dot_general(qd, kd, NT_DIMS, preferred_element_type=F32)

    o_intra, upd, q_in, st_dec = [], [], [], []
    for c, rows in enumerate(chunks):
        base = c * CHUNK
        q = q_ref[rows, :].astype(F32)
        k = k_ref[rows, :]
        b = b_ref[rows, :]
        a = None
        for size in LEVELS:
            m = level[size, c]
            a = jnp.where(level_masks[size], m, 0.0) if a is None else jnp.where(level_masks[size], m, a)
        for s in range(SUB):
            kb = jnp.concatenate([_bcast_row(k_ref, base + SUB * j + s, SUB) for j in range(CHUNK // SUB)], axis=0)
            bb = jnp.concatenate([_bcast_row(b_ref, base + SUB * j + s, SUB) for j in range(CHUNK // SUB)], axis=0)
            col = jnp.sum(q * kb * jnp.exp(b - bb), axis=-1, keepdims=True)
            a = jnp.where(diag_masks[s], col, a)
        v_bf = v_ref[rows, :]
        o_intra.append(jnp.dot(a.astype(BF16), v_bf, preferred_element_type=F32))
        b_last = _bcast_row(b_ref, base + CHUNK - 1, CHUNK)
        kdec = (k * jnp.exp(b_last - b)).astype(BF16)
        upd.append(lax.dot_general(v_bf, kdec, TN_DIMS, preferred_element_type=F32))
        q_in.append((q * jnp.exp(b)).astype(BF16))
        st_dec.append(jnp.exp(b_last[0:1, :]))

    ng = ng_ref[...].astype(F32)
    st = st_ref[...]
    for c, rows in enumerate(chunks):
        o = o_intra[c] + lax.dot_general(q_in[c], st.astype(BF16), NT_DIMS, preferred_element_type=F32)
        st = st * st_dec[c] + upd[c]
        o = o * (D_HEAD ** -0.5)
        ms = jnp.mean(o * o, axis=-1, keepdims=True)
        on = o * lax.rsqrt(ms + EPS) * ng
        o_ref[rows, :] = (on * _silu(gate_ref[rows, :].astype(F32))).astype(o_ref.dtype)
    st_ref[...] = st


def _hgrn(proj, lb_logits, norm_g, *, layer, batch, seq, tb):
    nt = seq // tb
    depth = lb_logits.shape[0]

    def col_spec(slot):
        return pl.BlockSpec((tb, D_HEAD), lambda b, h, t: (b * nt + t, slot * H_A + h))

    return pl.pallas_call(
        functools.partial(_hgrn_kernel, layer=layer),
        out_shape=jax.ShapeDtypeStruct((batch * seq, H_A * D_HEAD), BF16),
        grid=(batch, H_A, nt),
        in_specs=[
            pl.BlockSpec((depth, D_HEAD), lambda b, h, t: (0, h)),
            col_spec(0), col_spec(1), col_spec(2), col_spec(3),
            pl.BlockSpec((1, D_HEAD), lambda b, h, t: (0, 0)),
        ],
        out_specs=pl.BlockSpec((tb, D_HEAD), lambda b, h, t: (b * nt + t, h)),
        scratch_shapes=[pltpu.VMEM((D_HEAD, D_HEAD), F32), pltpu.VMEM((tb, D_HEAD), F32),
                        pltpu.VMEM((tb, D_HEAD), F32)],
        compiler_params=pltpu.CompilerParams(
            dimension_semantics=("parallel", "parallel", "arbitrary"), vmem_limit_bytes=VMEM_LIMIT),
        name="hgrn2",
    )(lb_logits, proj, proj, proj, proj, norm_g)


HIST = 32
LANES = 128
STRIP = 64


def _conv_kernel(u_ref, ug_ref, gate_ref, w_ref, bias_ref, lng_ref, lnb_ref, o_ref, a_ref, y_ref):
    tb, wb = u_ref.shape

    @pl.when(pl.program_id(1) == 0)
    def _():
        a_ref[0:HIST, :] = jnp.zeros((HIST, wb), F32)

    first = HIST - (CONV_WIDTH - 1)
    ext = STRIP + SUB

    def strip(si, carry):
        r0 = pl.multiple_of(si * STRIP, STRIP)
        rows = pl.ds(r0, STRIP)
        u = u_ref[rows, :].astype(F32)
        ug = ug_ref[rows, :].astype(F32)
        a_ref[pl.ds(r0 + HIST, STRIP), :] = u * (1.0 / (1.0 + jnp.exp(-ug)))

        s1 = jnp.zeros((STRIP, LANES), F32)
        for cs in range(0, wb, LANES):
            cols = pl.ds(cs, LANES)
            acc = jnp.broadcast_to(bias_ref[0:1, cols], (STRIP, LANES)).astype(F32)
            for m in range(SUB):
                nrows = STRIP if m == 0 else ext
                z = None
                for p in range((first + CONV_WIDTH - 1) // SUB + 1):
                    j = SUB * p + m - first
                    if 0 <= j < CONV_WIDTH:
                        term = w_ref[j:j + 1, cols] * a_ref[pl.ds(r0 + SUB * p, nrows), cols]
                        z = term if z is None else z + term
                acc = acc + (z if m == 0 else pltpu.roll(z, ext - m, 0)[:STRIP])
            y_ref[rows, cols] = acc
            s1 = s1 + acc
        mu = jnp.sum(s1, axis=-1, keepdims=True) * (1.0 / wb)
        yc = y_ref[rows, :] - mu
        var = jnp.mean(yc * yc, axis=-1, keepdims=True)
        yn = yc * lax.rsqrt(var + EPS) * lng_ref[...] + lnb_ref[...]
        gate = gate_ref[rows, :].astype(F32)
        o_ref[rows, :] = (_silu(yn) * _silu(gate)).astype(o_ref.dtype)
        return carry

    lax.fori_loop(0, tb // STRIP, strip, 0)

    a_ref[0:HIST, :] = a_ref[tb:tb + HIST, :]


def _conv(proj, w, bias, ln_g, ln_b, *, batch, seq, tb):
    nt = seq // tb
    wb = w.shape[1]
    base = 4 * H_A * D_HEAD // wb

    def col_spec(slot):
        return pl.BlockSpec((tb, wb), lambda b, t: (b * nt + t, base + slot))

    def full(shape):
        return pl.BlockSpec(shape, lambda b, t: (0, 0))

    return pl.pallas_call(
        _conv_kernel,
        out_shape=jax.ShapeDtypeStruct((batch * seq, wb), BF16),
        grid=(batch, nt),
        in_specs=[col_spec(0), col_spec(1), col_spec(2),
                  full((CONV_WIDTH, wb)), full((1, wb)), full((1, wb)), full((1, wb))],
        out_specs=pl.BlockSpec((tb, wb), lambda b, t: (b * nt + t, 0)),
        scratch_shapes=[pltpu.VMEM((HIST + tb, wb), F32), pltpu.VMEM((tb, wb), F32)],
        compiler_params=pltpu.CompilerParams(
            dimension_semantics=("parallel", "arbitrary"), vmem_limit_bytes=VMEM_LIMIT),
        name="conf_conv",
    )(proj, proj, proj, w, bias, ln_g, ln_b)


def _out_proj_kernel(oa_ref, ob_ref, wa_ref, wb_ref, x_ref, g_ref, o_ref):
    y = jnp.dot(oa_ref[...], wa_ref[...], preferred_element_type=F32)
    y = y + jnp.dot(ob_ref[...], wb_ref[...], preferred_element_type=F32)
    ms = jnp.mean(y * y, axis=-1, keepdims=True)
    o_ref[...] = x_ref[...] + y * lax.rsqrt(ms + EPS) * g_ref[...]


def _out_proj(oa, ob, w_bf16, x2, g, *, tm):
    m, d = x2.shape
    wa = oa.shape[1]
    wb = ob.shape[1]
    return pl.pallas_call(
        _out_proj_kernel,
        out_shape=jax.ShapeDtypeStruct((m, d), F32),
        grid=(m // tm,),
        in_specs=[
            pl.BlockSpec((tm, wa), lambda i: (i, 0)),
            pl.BlockSpec((tm, wb), lambda i: (i, 0)),
            pl.BlockSpec((wa, d), lambda i: (0, 0)),
            pl.BlockSpec((wb, d), lambda i: (wa // wb, 0)),
            pl.BlockSpec((tm, d), lambda i: (i, 0)),
            pl.BlockSpec((1, d), lambda i: (0, 0)),
        ],
        out_specs=pl.BlockSpec((tm, d), lambda i: (i, 0)),
        compiler_params=pltpu.CompilerParams(
            dimension_semantics=("parallel",), vmem_limit_bytes=VMEM_LIMIT),
        name="out_proj",
    )(oa, ob, w_bf16, w_bf16, x2, g)


def _tiles(batch, seq):
    m = batch * seq
    return dict(
        tm_in=min(1024, m), tn_in=1024,
        tb_hgrn=min(512, seq), tb_conv=min(512, seq),
        tm_out=min(512, m),
    )


def kernel(x, pre_norm_g, post_norm_g, w_in, lb_logits, hg_norm_g, dw_kernel, dw_bias, conv_ln_g,
           conv_ln_b, w_out):
    batch, seq, d = x.shape
    depth = w_in.shape[0]
    t = _tiles(batch, seq)
    x2 = x.reshape(batch * seq, d)
    for l in range(depth):
        proj = _in_proj(x2, pre_norm_g[l:l + 1], w_in[l].astype(BF16), tm=t["tm_in"], tn=t["tn_in"])
        oa = _hgrn(proj, lb_logits, hg_norm_g[l:l + 1], layer=l, batch=batch, seq=seq, tb=t["tb_hgrn"])
        ob = _conv(proj, dw_kernel[l], dw_bias[l:l + 1], conv_ln_g[l:l + 1], conv_ln_b[l:l + 1],
                   batch=batch, seq=seq, tb=t["tb_conv"])
        x2 = _out_proj(oa, ob, w_out[l].astype(BF16), x2, post_norm_g[l:l + 1], tm=t["tm_out"])
    return x2.reshape(batch, seq, d)
```
